```python
import jax, jax.numpy as jnp
from jax import lax
import numpy as np

D_MODEL = 1024
BATCH = 8
SEQ = 8192
DEPTH = 2

MEM_LEN = 256
EPS = 1e-6

CHUNK = 128
A_GROUPS = 4
A_WIDTH = D_MODEL // 2
A_HEAD = A_WIDTH // A_GROUPS
B_WIDTH = D_MODEL // 2
CONV_WIDTH = 3
POOL_WINDOWS = (2, 4, 8, 16)
C_WIDTH = D_MODEL // 2
C_GROUP = C_WIDTH // len(POOL_WINDOWS)
N_BRANCH = 3
IN_SPLITS = (2 * A_WIDTH, 2 * A_WIDTH + 3 * B_WIDTH, 2 * A_WIDTH + 3 * B_WIDTH + C_WIDTH)
IN_COLS = 2 * A_WIDTH + 3 * B_WIDTH + C_WIDTH + N_BRANCH * D_MODEL
XATTN_HEADS = 4
XATTN_HEAD_DIM = D_MODEL // XATTN_HEADS
FFN_HIDDEN = ((-(-8 * D_MODEL // 3) + 255) // 256) * 256

kernel_name = "hybrid_gated_gmlp_conv_pool_xattn_block"


def rmsnorm(x, g):
    xf = x.astype(jnp.float32)
    y = xf * lax.rsqrt(jnp.mean(xf * xf, axis=-1, keepdims=True) + EPS)
    return (y * g.astype(jnp.float32)).astype(x.dtype)


def layernorm(x, g, b):
    xf = x.astype(jnp.float32)
    mu = jnp.mean(xf, axis=-1, keepdims=True)
    var = jnp.mean(jnp.square(xf - mu), axis=-1, keepdims=True)
    y = (xf - mu) * lax.rsqrt(var + EPS)
    return (y * g.astype(jnp.float32) + b.astype(jnp.float32)).astype(x.dtype)


def chunked_spatial_gating(z, ln_g, ln_b, ws, bs):
    u, v = jnp.split(z, 2, axis=-1)
    v = layernorm(v, ln_g, ln_b)
    nb, s, _ = v.shape
    v = v.reshape(nb, s // CHUNK, CHUNK, A_GROUPS, A_HEAD)
    causal = jnp.tril(jnp.ones((CHUNK, CHUNK), dtype=bool))
    w = jnp.where(causal[None], ws, jnp.zeros_like(ws)).astype(v.dtype)
    mixed = jnp.einsum('gij,bcjgd->bcigd', w, v) + bs.T.astype(v.dtype)[None, None, :, :, None]
    return u * mixed.reshape(nb, s, A_WIDTH)


def short_gated_conv(z, w_conv):
    gb, gc, xin = jnp.split(z, 3, axis=-1)
    y = gc * xin
    k = w_conv[:, None, :].astype(y.dtype)
    y = lax.conv_general_dilated(y, k, window_strides=(1,), padding=[(CONV_WIDTH - 1, 0)],
                                 dimension_numbers=('NWC', 'WIO', 'NWC'),
                                 feature_group_count=B_WIDTH)
    return gb * y


def multiscale_pool(z, wg, scale):
    nb, s, _ = z.shape
    zf = z.astype(jnp.float32)
    cs = jnp.concatenate([jnp.zeros((nb, 1, C_WIDTH), jnp.float32), jnp.cumsum(zf, axis=1)], axis=1)
    pos = jnp.arange(s)
    outs = []
    for gi, win in enumerate(POOL_WINDOWS):
        sl = slice(gi * C_GROUP, (gi + 1) * C_GROUP)
        seg = cs[:, :, sl]
        hi = seg[:, 1:]
        lo = jnp.concatenate([jnp.zeros((nb, win - 1, C_GROUP), jnp.float32), seg[:, :s - win + 1]], axis=1)
        count = jnp.minimum(pos + 1, win).astype(jnp.float32)[None, :, None]
        outs.append((hi - lo) / count - zf[:, :, sl])
    p = jnp.stack(outs, axis=2)
    p = jnp.einsum('bsgc,gcd->bsgd', p, wg.astype(jnp.float32)).reshape(nb, s, C_WIDTH)
    return (p * scale.astype(jnp.float32)).astype(z.dtype)


def mixer_block(x, g_mix, w_in, b_gate, a_ln_g, a_ln_b, a_ws, a_bs, b_conv, c_wg, c_scale,
                w_branch_a, w_branch_b, w_branch_c, w_o):
    nb, s, _ = x.shape
    h = rmsnorm(x, g_mix)
    proj = h @ w_in
    za, zb, zc, zg = jnp.split(proj, IN_SPLITS, axis=-1)
    ya = chunked_spatial_gating(jax.nn.gelu(za, approximate=False), a_ln_g, a_ln_b, a_ws, a_bs) @ w_branch_a
    yb = short_gated_conv(zb, b_conv) @ w_branch_b
    yc = multiscale_pool(zc, c_wg, c_scale) @ w_branch_c
    gates = jax.nn.sigmoid(zg.reshape(nb, s, N_BRANCH, D_MODEL) + b_gate)
    merged = gates[:, :, 0] * ya + gates[:, :, 1] * yb + gates[:, :, 2] * yc
    return x + merged @ w_o


def cross_attention(x, mem, g_x, g_m, w_q, w_kv, w_o):
    nb, s, _ = x.shape
    hq = rmsnorm(x, g_x)
    m = rmsnorm(mem, g_m)
    q = (hq @ w_q).reshape(nb, s, XATTN_HEADS, XATTN_HEAD_DIM)
    k, v = jnp.split(m @ w_kv, 2, axis=-1)
    k = k.reshape(nb, MEM_LEN, XATTN_HEADS, XATTN_HEAD_DIM)
    v = v.reshape(nb, MEM_LEN, XATTN_HEADS, XATTN_HEAD_DIM)
    sc = jnp.einsum('bshd,bmhd->bhsm', q, k).astype(jnp.float32) * (XATTN_HEAD_DIM ** -0.5)
    p = jax.nn.softmax(sc, axis=-1).astype(v.dtype)
    o = jnp.einsum('bhsm,bmhd->bshd', p, v).reshape(nb, s, D_MODEL)
    return x + o @ w_o


def swiglu_ffn(x, g, w_in, w_out):
    h = rmsnorm(x, g)
    gate, up = jnp.split(h @ w_in, 2, axis=-1)
    return x + (jax.nn.silu(gate) * up) @ w_out


def setup_inputs(seed: int = 0) -> dict:
    key = jax.random.key(seed)
    ks = jax.random.split(key, 32)
    f32 = jnp.float32
    L, D = DEPTH, D_MODEL

    def nrm(k, shape, fan_in):
        return jax.random.normal(k, shape, f32) * (fan_in ** -0.5)

    def gain(k, shape):
        return 1.0 + 0.02 * jax.random.normal(k, shape, f32)

    def small(k, shape, s=0.02):
        return s * jax.random.normal(k, shape, f32)

    return {
        "x": jax.random.normal(ks[0], (BATCH, SEQ, D), f32),
        "mem": jax.random.normal(ks[1], (BATCH, MEM_LEN, D), f32),
        "g_mix": gain(ks[2], (L, D)),
        "w_in": nrm(ks[3], (L, D, IN_COLS), D),
        "b_gate": small(ks[4], (L, N_BRANCH, D), 0.01),
        "a_ln_g": gain(ks[5], (L, A_WIDTH)),
        "a_ln_b": small(ks[6], (L, A_WIDTH)),
        "a_ws": nrm(ks[7], (L, A_GROUPS, CHUNK, CHUNK), CHUNK),
        "a_bs": gain(ks[8], (L, A_GROUPS, CHUNK)),
        "b_conv": nrm(ks[9], (L, CONV_WIDTH, B_WIDTH), CONV_WIDTH),
        "c_wg": nrm(ks[10], (L, len(POOL_WINDOWS), C_GROUP, C_GROUP), C_GROUP),
        "c_scale": gain(ks[11], (L, C_WIDTH)),
        "w_branch_a": nrm(ks[12], (L, A_WIDTH, D), A_WIDTH),
        "w_branch_b": nrm(ks[13], (L, B_WIDTH, D), B_WIDTH),
        "w_branch_c": nrm(ks[14], (L, C_WIDTH, D), C_WIDTH),
        "w_o": nrm(ks[15], (L, D, D), D),
        "g_xattn": gain(ks[16], (L, D)),
        "g_mem": gain(ks[17], (L, D)),
        "w_xq": nrm(ks[18], (L, D, D), D),
        "w_xkv": nrm(ks[19], (L, D, 2 * D), D),
        "w_xo": nrm(ks[20], (L, D, D), D),
        "g_ffn": gain(ks[21], (L, D)),
        "w_ffn_in": nrm(ks[22], (L, D, 2 * FFN_HIDDEN), D),
        "w_ffn_out": nrm(ks[23], (L, FFN_HIDDEN, D), FFN_HIDDEN),
        "g_final": gain(ks[24], (D,)),
    }


def reference(x, mem, g_mix, w_in, b_gate, a_ln_g, a_ln_b, a_ws, a_bs, b_conv, c_wg, c_scale,
              w_branch_a, w_branch_b, w_branch_c, w_o, g_xattn, g_mem, w_xq, w_xkv, w_xo,
              g_ffn, w_ffn_in, w_ffn_out, g_final):
    for l in range(DEPTH):
        x = mixer_block(x, g_mix[l], w_in[l], b_gate[l], a_ln_g[l], a_ln_b[l], a_ws[l], a_bs[l],
                        b_conv[l], c_wg[l], c_scale[l], w_branch_a[l], w_branch_b[l], w_branch_c[l], w_o[l])
        x = cross_attention(x, mem, g_xattn[l], g_mem[l], w_xq[l], w_xkv[l], w_xo[l])
        x = swiglu_ffn(x, g_ffn[l], w_ffn_in[l], w_ffn_out[l])
    return rmsnorm(x, g_final)
```

```python
import functools
import math

import jax
import jax.numpy as jnp
from jax import lax
from jax.experimental import pallas as pl
from jax.experimental.pallas import tpu as pltpu

EPS = 1e-6
CHUNK = 128
A_GROUPS = 4
POOL_WINDOWS = (2, 4, 8, 16)
CONV_WIDTH = 3
XATTN_HEADS = 4

LANES = 128
SUBLANES = 8
V7X_VMEM_BYTES = 64 * 1024 * 1024
VMEM_LIMIT_BYTES = V7X_VMEM_BYTES - 8 * 1024 * 1024

TOKEN_TILE = 512
POOL_HALO = 32
CONV_HALO = SUBLANES

BF16 = jnp.bfloat16
F32 = jnp.float32


def _dot(a, b):
    return jnp.dot(a, b, preferred_element_type=F32)


def _rmsnorm(x, g):
    return x * lax.rsqrt(jnp.mean(x * x, axis=-1, keepdims=True) + EPS) * g


def _sigmoid(x):
    return 1.0 / (1.0 + jnp.exp(-x))


def _mixer_kernel(x_ref, g_ref, w_in_ref, bg_ref, lng_ref, lnb_ref, ws_ref, bsb_ref, conv_ref,
                  wg_ref, cs_ref, wa_ref, wb_ref, wc_ref, wo_ref, o_ref,
                  ybuf, cbuf, l1, l2, l3, *, tiles_per_batch):
    tm, d = x_ref.shape
    aw = d // 2
    hd = aw // A_GROUPS
    nch = tm // CHUNK
    t_in_batch = pl.program_id(0) % tiles_per_batch

    @pl.when(t_in_batch == 0)
    def _():
        ybuf[0:CONV_HALO, :] = jnp.zeros((CONV_HALO, aw), F32)
        cbuf[0:POOL_HALO, :] = jnp.zeros((POOL_HALO, aw), F32)

    x = x_ref[...]
    h = _rmsnorm(x, g_ref[...]).astype(BF16)

    za = _dot(h, w_in_ref[:, 0:2 * aw])
    a = 0.5 * za * (1.0 + lax.erf(za * (1.0 / math.sqrt(2.0))))
    u = a[:, :aw]
    v = a[:, aw:]
    mu = jnp.mean(v, axis=-1, keepdims=True)
    vc = v - mu
    var = jnp.mean(vc * vc, axis=-1, keepdims=True)
    vb = (vc * lax.rsqrt(var + EPS) * lng_ref[...] + lnb_ref[...]).astype(BF16)
    row = lax.broadcasted_iota(jnp.int32, (CHUNK, CHUNK), 0)
    col = lax.broadcasted_iota(jnp.int32, (CHUNK, CHUNK), 1)
    causal = row >= col
    mixed_g = []
    for g in range(A_GROUPS):
        w = jnp.where(causal, ws_ref[g], 0.0).astype(BF16)
        vcat = jnp.concatenate(
            [vb[c * CHUNK:(c + 1) * CHUNK, g * hd:(g + 1) * hd] for c in range(nch)], axis=1)
        mixed_g.append(_dot(w, vcat))
    bias = bsb_ref[...]
    mixed = jnp.concatenate(
        [jnp.concatenate([mixed_g[g][:, c * hd:(c + 1) * hd] for g in range(A_GROUPS)], axis=1)
         + bias for c in range(nch)], axis=0)
    ya = _dot((u * mixed).astype(BF16), wa_ref[...])

    zb = _dot(h, w_in_ref[:, 2 * aw:5 * aw])
    y = zb[:, aw:2 * aw] * zb[:, 2 * aw:3 * aw]
    ybuf[CONV_HALO:CONV_HALO + tm, :] = y
    conv = (conv_ref[0:1, :] * ybuf[CONV_HALO - 2:CONV_HALO - 2 + tm, :]
            + conv_ref[1:2, :] * ybuf[CONV_HALO - 1:CONV_HALO - 1 + tm, :]
            + conv_ref[2:3, :] * y)
    ybuf[0:CONV_HALO, :] = y[tm - CONV_HALO:tm, :]
    yb = _dot((zb[:, 0:aw] * conv).astype(BF16), wb_ref[...])

    zc = _dot(h, w_in_ref[:, 5 * aw:6 * aw])
    e = tm + POOL_HALO
    cbuf[POOL_HALO:e, :] = zc
    s2 = cbuf[8:e, :] + cbuf[7:e - 1, :]
    l1[8:e, :] = s2
    s4 = l1[16:e, hd:] + l1[14:e - 2, hd:]
    l2[16:e, hd:] = s4
    s8 = l2[24:e, 2 * hd:] + l2[20:e - 4, 2 * hd:]
    l3[24:e, 3 * hd:] = s8[:, hd:]
    s16 = l3[32:e, 3 * hd:] + l3[24:e - 8, 3 * hd:]
    sums = (s2[POOL_HALO - 8:, 0:hd], s4[POOL_HALO - 16:, 0:hd], s8[POOL_HALO - 24:, 0:hd], s16)
    cbuf[0:POOL_HALO, :] = zc[tm - POOL_HALO:tm, :]
    pos = t_in_batch * tm + lax.broadcasted_iota(jnp.int32, (tm, hd), 0)
    pooled = []
    for gi, win in enumerate(POOL_WINDOWS):
        inv_count = 1.0 / jnp.minimum(pos + 1, win).astype(F32)
        p = sums[gi] * inv_count - zc[:, gi * hd:(gi + 1) * hd]
        pooled.append(_dot(p.astype(BF16), wg_ref[gi]))
    pc = jnp.concatenate(pooled, axis=1) * cs_ref[...]
    yc = _dot(pc.astype(BF16), wc_ref[...])

    merged = None
    for k, yk in enumerate((ya, yb, yc)):
        zg = _dot(h, w_in_ref[:, (6 + 2 * k) * aw:(8 + 2 * k) * aw]) + bg_ref[k:k + 1, :]
        term = _sigmoid(zg) * yk
        merged = term if merged is None else merged + term
    o_ref[...] = x + _dot(merged.astype(BF16), wo_ref[...])


def _kv_kernel(mem_ref, g_ref, wkv_ref, k_ref, v_ref):
    d = mem_ref.shape[-1]
    m = _rmsnorm(mem_ref[...], g_ref[...]).astype(BF16)
    kv = _dot(m, wkv_ref[...])
    k_ref[...] = kv[:, :d].astype(BF16)
    v_ref[...] = kv[:, d:].astype(BF16)


def _xattn_kernel(x_ref, g_ref, wq_ref, k_ref, v_ref, wo_ref, o_ref):
    tm, d = x_ref.shape
    dh = d // XATTN_HEADS
    x = x_ref[...]
    hq = _rmsnorm(x, g_ref[...]).astype(BF16)
    q = (_dot(hq, wq_ref[...]) * (dh ** -0.5)).astype(BF16)
    heads = []
    for hh in range(XATTN_HEADS):
        sl = slice(hh * dh, (hh + 1) * dh)
        s = lax.dot_general(q[:, sl], k_ref[:, sl], (((1,), (1,)), ((), ())),
                            preferred_element_type=F32)
        p = jnp.exp(s - jnp.max(s, axis=-1, keepdims=True))
        inv_l = 1.0 / jnp.sum(p, axis=-1, keepdims=True)
        heads.append(_dot(p.astype(BF16), v_ref[:, sl]) * inv_l)
    o = jnp.concatenate(heads, axis=1).astype(BF16)
    o_ref[...] = x + _dot(o, wo_ref[...])


def _ffn_kernel(x_ref, g_ref, w_in_ref, w_out_ref, gf_ref, o_ref, *, final_norm):
    hidden = w_out_ref.shape[0]
    x = x_ref[...]
    h = _rmsnorm(x, g_ref[...]).astype(BF16)
    gate = _dot(h, w_in_ref[:, 0:hidden])
    up = _dot(h, w_in_ref[:, hidden:2 * hidden])
    act = (gate * _sigmoid(gate) * up).astype(BF16)
    out = x + _dot(act, w_out_ref[...])
    if final_norm:
        out = _rmsnorm(out, gf_ref[...])
    o_ref[...] = out


def _layer_spec(arr, l):
    nd = arr.ndim - 1
    return pl.BlockSpec((None,) + arr.shape[1:], lambda i: (l,) + (0,) * nd,
                        pipeline_mode=pl.Buffered(1))


def _compiler_params():
    return pltpu.CompilerParams(dimension_semantics=("arbitrary",),
                                vmem_limit_bytes=VMEM_LIMIT_BYTES)


def _mixer_call(x2, l, seq, g_mix, w_in, b_gate, ln_g, ln_b, ws, bsb, conv, wg, cs, wa, wb, wc, wo):
    n, d = x2.shape
    tm = TOKEN_TILE
    aw = d // 2
    tok = pl.BlockSpec((tm, d), lambda i: (i, 0))
    params = (g_mix, w_in, b_gate, ln_g, ln_b, ws, bsb, conv, wg, cs, wa, wb, wc, wo)
    return pl.pallas_call(
        functools.partial(_mixer_kernel, tiles_per_batch=seq // tm),
        grid=(n // tm,),
        in_specs=[tok] + [_layer_spec(p, l) for p in params],
        out_specs=tok,
        out_shape=jax.ShapeDtypeStruct((n, d), F32),
        scratch_shapes=[pltpu.VMEM((tm + CONV_HALO, aw), F32)]
        + [pltpu.VMEM((tm + POOL_HALO, aw), F32)] * 4,
        compiler_params=_compiler_params(),
        name=f"mixer_l{l}",
    )(x2, *params)


def _kv_call(mem, l, g_mem, w_kv):
    b, m, d = mem.shape
    blk = pl.BlockSpec((None, m, d), lambda i: (i, 0, 0))
    return pl.pallas_call(
        _kv_kernel,
        grid=(b,),
        in_specs=[blk, _layer_spec(g_mem, l), _layer_spec(w_kv, l)],
        out_specs=[blk, blk],
        out_shape=[jax.ShapeDtypeStruct((b, m, d), BF16)] * 2,
        compiler_params=_compiler_params(),
        name=f"kv_l{l}",
    )(mem, g_mem, w_kv)


def _xattn_call(x2, l, seq, k, v, g_x, wq, wo):
    n, d = x2.shape
    tm = TOKEN_TILE
    tiles_per_batch = seq // tm
    tok = pl.BlockSpec((tm, d), lambda i: (i, 0))
    kvb = pl.BlockSpec((None,) + k.shape[1:], lambda i: (i // tiles_per_batch, 0, 0))
    return pl.pallas_call(
        _xattn_kernel,
        grid=(n // tm,),
        in_specs=[tok, _layer_spec(g_x, l), _layer_spec(wq, l), kvb, kvb, _layer_spec(wo, l)],
        out_specs=tok,
        out_shape=jax.ShapeDtypeStruct((n, d), F32),
        compiler_params=_compiler_params(),
        name=f"xattn_l{l}",
    )(x2, g_x, wq, k, v, wo)


def _ffn_call(x2, l, g_ffn, w_in, w_out, g_final, final_norm):
    n, d = x2.shape
    tm = TOKEN_TILE
    tok = pl.BlockSpec((tm, d), lambda i: (i, 0))
    gf = pl.BlockSpec(g_final.shape, lambda i: (0, 0), pipeline_mode=pl.Buffered(1))
    return pl.pallas_call(
        functools.partial(_ffn_kernel, final_norm=final_norm),
        grid=(n // tm,),
        in_specs=[tok, _layer_spec(g_ffn, l), _layer_spec(w_in, l), _layer_spec(w_out, l), gf],
        out_specs=tok,
        out_shape=jax.ShapeDtypeStruct((n, d), F32),
        compiler_params=_compiler_params(),
        name=f"ffn_l{l}",
    )(x2, g_ffn, w_in, w_out, g_final)


def kernel(x, mem, g_mix, w_in, b_gate, a_ln_g, a_ln_b, a_ws, a_bs, b_conv, c_wg, c_scale,
           w_branch_a, w_branch_b, w_branch_c, w_o, g_xattn, g_mem, w_xq, w_xkv, w_xo,
           g_ffn, w_ffn_in, w_ffn_out, g_final):
    b, s, d = x.shape
    depth = w_in.shape[0]
    assert s % TOKEN_TILE == 0 and TOKEN_TILE % CHUNK == 0 and TOKEN_TILE >= POOL_HALO
    assert d // 2 == A_GROUPS * LANES == len(POOL_WINDOWS) * LANES and a_ws.shape[-1] == CHUNK

    row = lambda p: p[:, None, :]
    bsb = jnp.repeat(jnp.swapaxes(a_bs, 1, 2), LANES, axis=2)
    w_in_b, wa_b, wb_b, wc_b, wo_b = (w.astype(BF16) for w in (w_in, w_branch_a, w_branch_b, w_branch_c, w_o))
    wg_b, wq_b, wkv_b, wxo_b = (w.astype(BF16) for w in (c_wg, w_xq, w_xkv, w_xo))
    wfi_b, wfo_b = w_ffn_in.astype(BF16), w_ffn_out.astype(BF16)

    x2 = x.reshape(b * s, d)
    for l in range(depth):
        x2 = _mixer_call(x2, l, s, row(g_mix), w_in_b, b_gate, row(a_ln_g), row(a_ln_b), a_ws, bsb,
                         b_conv, wg_b, row(c_scale), wa_b, wb_b, wc_b, wo_b)
        k, v = _kv_call(mem, l, row(g_mem), wkv_b)
        x2 = _xattn_call(x2, l, s, k, v, row(g_xattn), wq_b, wxo_b)
        x2 = _ffn_call(x2, l, row(g_ffn), wfi_b, wfo_b, g_final[None, :], l == depth - 1)
    return x2.reshape(b, s, d)
```

```python
import functools
import math

import jax
import jax.numpy as jnp
from jax import lax
from jax.experimental import pallas as pl
from jax.experimental.pallas import tpu as pltpu

EPS = 1e-6
CHUNK = 128
A_GROUPS = 4
POOL_WINDOWS = (2, 4, 8, 16)
CONV_WIDTH = 3
XATTN_HEADS = 4

LANES = 128
SUBLANES = 8
V7X_VMEM_BYTES = 64 * 1024 * 1024
VMEM_LIMIT_BYTES = V7X_VMEM_BYTES - 8 * 1024 * 1024

MIXER_SUB = 256
MIXER_TILE = 512
PAIR_SUB = 512
PAIR_TILE = 1024
POOL_HALO = 32
CONV_HALO = SUBLANES

BF16 = jnp.bfloat16
F32 = jnp.float32

_dot = functools.partial(jnp.dot, preferred_element_type=F32)


def _rmsnorm(x, g):
    return x * lax.rsqrt(jnp.mean(x * x, axis=-1, keepdims=True) + EPS) * g


def _sigmoid(x):
    return 1.0 / (1.0 + jnp.exp(-x))


def _gelu(z):
    return 0.5 * z * (1.0 + lax.erf(z * (1.0 / math.sqrt(2.0))))


def _interleave(n_sub, stages, skew=1):
    for wave in range(skew * (n_sub - 1) + len(stages)):
        for s in reversed(range(n_sub)):
            k = wave - skew * s
            if 0 <= k < len(stages):
                stages[k](s)


def _mixer_kernel(x_ref, g_ref, w_in_ref, bg_ref, lng_ref, lnb_ref, ws_ref, bsb_ref, conv_ref,
                  wg_ref, cs_ref, wa_ref, wb_ref, wc_ref, wo_ref, o_ref,
                  ybuf, cbuf, l1, l2, l3, *, tiles_per_batch):
    tm, d = x_ref.shape
    sub = MIXER_SUB
    aw = d // 2
    hd = aw // A_GROUPS
    nch = sub // CHUNK
    t_in_batch = pl.program_id(0) % tiles_per_batch

    @pl.when(t_in_batch == 0)
    def _():
        ybuf[0:CONV_HALO, :] = jnp.zeros((CONV_HALO, aw), F32)
        cbuf[0:POOL_HALO, :] = jnp.zeros((POOL_HALO, aw), F32)

    row = lax.broadcasted_iota(jnp.int32, (CHUNK, CHUNK), 0)
    col = lax.broadcasted_iota(jnp.int32, (CHUNK, CHUNK), 1)
    ws = [jnp.where(row >= col, ws_ref[g], 0.0).astype(BF16) for g in range(A_GROUPS)]
    zero = jnp.zeros((hd, hd), BF16)
    wbd = [jnp.concatenate([jnp.concatenate([wg_ref[gi], zero], axis=1),
                            jnp.concatenate([zero, wg_ref[gi + 1]], axis=1)], axis=0)
           for gi in range(0, len(POOL_WINDOWS), 2)]
    st = [dict() for _ in range(tm // sub)]

    def gate(h, k, lo, hi):
        zg = _dot(h, w_in_ref[:, 6 * aw + k * d + lo:6 * aw + k * d + hi])
        return _sigmoid(zg + bg_ref[k:k + 1, lo:hi])

    def in_proj(s):
        t = st[s]
        t["h"] = h = _rmsnorm(x_ref[s * sub:(s + 1) * sub, :], g_ref[...]).astype(BF16)
        t["zv"] = _dot(h, w_in_ref[:, aw:2 * aw])
        t["zu"] = _dot(h, w_in_ref[:, 0:aw])
        t["zb"] = _dot(h, w_in_ref[:, 2 * aw:5 * aw])

    def mix(s):
        t = st[s]
        h = t["h"]
        v = _gelu(t.pop("zv"))
        mu = jnp.mean(v, axis=-1, keepdims=True)
        vc = v - mu
        var = jnp.mean(vc * vc, axis=-1, keepdims=True)
        vb = (vc * lax.rsqrt(var + EPS) * lng_ref[...] + lnb_ref[...]).astype(BF16)
        mixed_g = []
        for g in range(A_GROUPS):
            vcat = jnp.concatenate(
                [vb[c * CHUNK:(c + 1) * CHUNK, g * hd:(g + 1) * hd] for c in range(nch)], axis=1)
            mixed_g.append(_dot(ws[g], vcat))
        zc = _dot(h, w_in_ref[:, 5 * aw:6 * aw])
        t["g0"] = gate(h, 0, 0, d)
        bias = bsb_ref[...]
        mixed = jnp.concatenate(
            [jnp.concatenate([mixed_g[g][:, c * hd:(c + 1) * hd] for g in range(A_GROUPS)], axis=1)
             + bias for c in range(nch)], axis=0)
        t["ua"] = (_gelu(t.pop("zu")) * mixed).astype(BF16)

        zb = t.pop("zb")
        y = zb[:, aw:2 * aw] * zb[:, 2 * aw:3 * aw]
        r0 = CONV_HALO + s * sub
        ybuf[r0:r0 + sub, :] = y
        conv = (conv_ref[0:1, :] * ybuf[r0 - 2:r0 - 2 + sub, :]
                + conv_ref[1:2, :] * ybuf[r0 - 1:r0 - 1 + sub, :]
                + conv_ref[2:3, :] * y)
        t["ybp"] = (zb[:, 0:aw] * conv).astype(BF16)

        r0 = POOL_HALO + s * sub
        r1 = r0 + sub
        cbuf[r0:r1, :] = zc
        lo = [8, 16, 24, 32] if s == 0 else [r0] * 4
        s2 = cbuf[lo[0]:r1, :] + cbuf[lo[0] - 1:r1 - 1, :]
        l1[lo[0]:r1, :] = s2[:, hd:]
        s4 = l1[lo[1]:r1, :] + l1[lo[1] - 2:r1 - 2, :]
        l2[lo[1]:r1, :] = s4[:, hd:]
        s8 = l2[lo[2]:r1, :] + l2[lo[2] - 4:r1 - 4, :]
        l3[lo[2]:r1, :] = s8[:, hd:]
        s16 = l3[lo[3]:r1, :] + l3[lo[3] - 8:r1 - 8, :]
        sums = (s2[r0 - lo[0]:, 0:hd], s4[r0 - lo[1]:, 0:hd], s8[r0 - lo[2]:, 0:hd], s16[r0 - lo[3]:, :])
        pos = t_in_batch * tm + s * sub + lax.broadcasted_iota(jnp.int32, (sub, hd), 0)
        pool = []
        for gi, win in enumerate(POOL_WINDOWS):
            inv_count = 1.0 / jnp.minimum(pos + 1, win).astype(F32)
            pool.append((sums[gi] * inv_count - zc[:, gi * hd:(gi + 1) * hd]).astype(BF16))
        t["pool"] = pool

    def branch_proj(s):
        t = st[s]
        h = t["h"]
        merged = t.pop("g0") * _dot(t.pop("ua"), wa_ref[...])
        yb = _dot(t.pop("ybp"), wb_ref[...])
        g1 = gate(h, 1, 0, d)
        pool = t.pop("pool")
        pooled = [_dot(jnp.concatenate(pool[2 * j:2 * j + 2], axis=1), wbd[j]) for j in range(len(wbd))]
        t["merged"] = merged + g1 * yb
        t["pc"] = (jnp.concatenate(pooled, axis=1) * cs_ref[...]).astype(BF16)

    def out_proj(s):
        t = st[s]
        h = t.pop("h")
        g2 = gate(h, 2, 0, d)
        yc = _dot(t.pop("pc"), wc_ref[...])
        merged = t.pop("merged") + g2 * yc
        rows = slice(s * sub, (s + 1) * sub)
        o_ref[rows, :] = x_ref[rows, :] + _dot(merged.astype(BF16), wo_ref[...])

    _interleave(len(st), (in_proj, mix, branch_proj, out_proj))

    ybuf[0:CONV_HALO, :] = ybuf[tm:tm + CONV_HALO, :]
    cbuf[0:POOL_HALO, :] = cbuf[tm:tm + POOL_HALO, :]


def _kv_kernel(mem_ref, g_ref, wkv_ref, k_ref, v_ref):
    d = mem_ref.shape[-1]
    m = _rmsnorm(mem_ref[...], g_ref[...]).astype(BF16)
    kv = _dot(m, wkv_ref[...])
    k_ref[...] = kv[:, :d].astype(BF16)
    v_ref[...] = kv[:, d:].astype(BF16)


def _xattn_kernel(x_ref, g_ref, wq_ref, k_ref, v_ref, wo_ref, o_ref):
    tm, d = x_ref.shape
    sub = PAIR_SUB
    dh = d // XATTN_HEADS
    head_cols = [slice(hh * dh, (hh + 1) * dh) for hh in range(XATTN_HEADS)]
    vals = [None] * (tm // sub)

    def q_proj(s):
        hq = _rmsnorm(x_ref[s * sub:(s + 1) * sub, :], g_ref[...]).astype(BF16)
        vals[s] = (_dot(hq, wq_ref[...]) * (dh ** -0.5)).astype(BF16)

    def probs(s):
        q = vals[s]
        out = []
        for sl in head_cols:
            sc = lax.dot_general(q[:, sl], k_ref[:, sl], (((1,), (1,)), ((), ())),
                                 preferred_element_type=F32)
            p = jnp.exp(sc - jnp.max(sc, axis=-1, keepdims=True))
            out.append((p.astype(BF16), 1.0 / jnp.sum(p, axis=-1, keepdims=True)))
        vals[s] = out

    def attend(s):
        heads = [_dot(p, v_ref[:, sl]) * inv_l for (p, inv_l), sl in zip(vals[s], head_cols)]
        vals[s] = jnp.concatenate(heads, axis=1).astype(BF16)

    def out_proj(s):
        rows = slice(s * sub, (s + 1) * sub)
        o_ref[rows, :] = x_ref[rows, :] + _dot(vals[s], wo_ref[...])
        vals[s] = None

    _interleave(len(vals), (q_proj, probs, attend, out_proj), skew=2)


def _ffn_kernel(x_ref, g_ref, w_in_ref, w_out_ref, gf_ref, o_ref, *, final_norm):
    tm, _ = x_ref.shape
    sub = PAIR_SUB
    hidden = w_out_ref.shape[0]
    acts = [None] * (tm // sub)

    def act(s):
        h = _rmsnorm(x_ref[s * sub:(s + 1) * sub, :], g_ref[...]).astype(BF16)
        gate = _dot(h, w_in_ref[:, 0:hidden])
        up = _dot(h, w_in_ref[:, hidden:2 * hidden])
        acts[s] = (gate * _sigmoid(gate) * up).astype(BF16)

    def out_proj(s):
        rows = slice(s * sub, (s + 1) * sub)
        out = x_ref[rows, :] + _dot(acts[s], w_out_ref[...])
        if final_norm:
            out = _rmsnorm(out, gf_ref[...])
        o_ref[rows, :] = out
        acts[s] = None

    _interleave(len(acts), (act, out_proj))


def _layer_spec(arr, l):
    nd = arr.ndim - 1
    return pl.BlockSpec((None,) + arr.shape[1:], lambda i: (l,) + (0,) * nd,
                        pipeline_mode=pl.Buffered(1))


def _compiler_params():
    return pltpu.CompilerParams(dimension_semantics=("arbitrary",),
                                vmem_limit_bytes=VMEM_LIMIT_BYTES)


def _mixer_call(x2, l, seq, g_mix, w_in, b_gate, ln_g, ln_b, ws, bsb, conv, wg, cs, wa, wb, wc, wo):
    n, d = x2.shape
    tm = MIXER_TILE
    aw = d // 2
    hd = aw // A_GROUPS
    tok = pl.BlockSpec((tm, d), lambda i: (i, 0))
    params = (g_mix, w_in, b_gate, ln_g, ln_b, ws, bsb, conv, wg, cs, wa, wb, wc, wo)
    pool_rows = tm + POOL_HALO
    return pl.pallas_call(
        functools.partial(_mixer_kernel, tiles_per_batch=seq // tm),
        grid=(n // tm,),
        in_specs=[tok] + [_layer_spec(p, l) for p in params],
        out_specs=tok,
        out_shape=jax.ShapeDtypeStruct((n, d), F32),
        scratch_shapes=[pltpu.VMEM((tm + CONV_HALO, aw), F32),
                        pltpu.VMEM((pool_rows, aw), F32),
                        pltpu.VMEM((pool_rows, aw - hd), F32),
                        pltpu.VMEM((pool_rows, aw - 2 * hd), F32),
                        pltpu.VMEM((pool_rows, aw - 3 * hd), F32)],
        compiler_params=_compiler_params(),
        name=f"mixer_l{l}",
    )(x2, *params)


def _kv_call(mem, l, g_mem, w_kv):
    b, m, d = mem.shape
    blk = pl.BlockSpec((None, m, d), lambda i: (i, 0, 0))
    return pl.pallas_call(
        _kv_kernel,
        grid=(b,),
        in_specs=[blk, _layer_spec(g_mem, l), _layer_spec(w_kv, l)],
        out_specs=[blk, blk],
        out_shape=[jax.ShapeDtypeStruct((b, m, d), BF16)] * 2,
        compiler_params=_compiler_params(),
        name=f"kv_l{l}",
    )(mem, g_mem, w_kv)


def _xattn_call(x2, l, seq, k, v, g_x, wq, wo):
    n, d = x2.shape
    tm = PAIR_TILE
    tiles_per_batch = seq // tm
    tok = pl.BlockSpec((tm, d), lambda i: (i, 0))
    kvb = pl.BlockSpec((None,) + k.shape[1:], lambda i: (i // tiles_per_batch, 0, 0))
    return pl.pallas_call(
        _xattn_kernel,
        grid=(n // tm,),
        in_specs=[tok, _layer_spec(g_x, l), _layer_spec(wq, l), kvb, kvb, _layer_spec(wo, l)],
        out_specs=tok,
        out_shape=jax.ShapeDtypeStruct((n, d), F32),
        compiler_params=_compiler_params(),
        name=f"xattn_l{l}",
    )(x2, g_x, wq, k, v, wo)


def _ffn_call(x2, l, g_ffn, w_in, w_out, g_final, final_norm):
    n, d = x2.shape
    tm = PAIR_TILE
    tok = pl.BlockSpec((tm, d), lambda i: (i, 0))
    gf = pl.BlockSpec(g_final.shape, lambda i: (0, 0), pipeline_mode=pl.Buffered(1))
    return pl.pallas_call(
        functools.partial(_ffn_kernel, final_norm=final_norm),
        grid=(n // tm,),
        in_specs=[tok, _layer_spec(g_ffn, l), _layer_spec(w_in, l), _layer_spec(w_out, l), gf],
        out_specs=tok,
        out_shape=jax.ShapeDtypeStruct((n, d), F32),
        compiler_params=_compiler_params(),
        name=f"ffn_l{l}",
    )(x2, g_ffn, w_in, w_out, g_final)


def kernel(x, mem, g_mix, w_in, b_gate, a_ln_g, a_ln_b, a_ws, a_bs, b_conv, c_wg, c_scale,
           w_branch_a, w_branch_b, w_branch_c, w_o, g_xattn, g_mem, w_xq, w_xkv, w_xo,
           g_ffn, w_ffn_in, w_ffn_out, g_final):
    b, s, d = x.shape
    depth = w_in.shape[0]
    assert s % PAIR_TILE == 0 and PAIR_TILE % PAIR_SUB == 0
    assert s % MIXER_TILE == 0 and MIXER_TILE % MIXER_SUB == 0 and MIXER_SUB % CHUNK == 0
    assert MIXER_SUB >= POOL_HALO
    assert d // 2 == A_GROUPS * LANES == len(POOL_WINDOWS) * LANES and a_ws.shape[-1] == CHUNK

    row = lambda p: p[:, None, :]
    bsb = jnp.repeat(jnp.swapaxes(a_bs, 1, 2), LANES, axis=2)
    w_in_b, wa_b, wb_b, wc_b, wo_b = (w.astype(BF16) for w in (w_in, w_branch_a, w_branch_b, w_branch_c, w_o))
    wg_b, wq_b, wkv_b, wxo_b = (w.astype(BF16) for w in (c_wg, w_xq, w_xkv, w_xo))
    wfi_b, wfo_b = w_ffn_in.astype(BF16), w_ffn_out.astype(BF16)

    x2 = x.reshape(b * s, d)
    for l in range(depth):
        x2 = _mixer_call(x2, l, s, row(g_mix), w_in_b, b_gate, row(a_ln_g), row(a_ln_b), a_ws, bsb,
                         b_conv, wg_b, row(c_scale), wa_b, wb_b, wc_b, wo_b)
        k, v = _kv_call(mem, l, row(g_mem), wkv_b)
        x2 = _xattn_call(x2, l, s, k, v, row(g_xattn), wq_b, wxo_b)
        x2 = _ffn_call(x2, l, row(g_ffn), wfi_b, wfo_b, g_final[None, :], l == depth - 1)
    return x2.reshape(b, s, d)
```

```python
import functools
import math

import jax
import jax.numpy as jnp
from jax import lax
from jax.experimental import pallas as pl
from jax.experimental.pallas import tpu as pltpu

EPS = 1e-6
CHUNK = 128
A_GROUPS = 4
POOL_WINDOWS = (2, 4, 8, 16)
CONV_WIDTH = 3
XATTN_HEADS = 4

LANES = 128
SUBLANES = 8
V7X_VMEM_BYTES = 64 * 1024 * 1024
VMEM_LIMIT_BYTES = V7X_VMEM_BYTES - 8 * 1024 * 1024

MIXER_SUB = 256
MIXER_TILE = 1024
XATTN_SUB = 512
XATTN_TILE = 2048
FFN_SUB = 512
FFN_TILE = 1024
POOL_HALO = 32
CONV_HALO = SUBLANES

BF16 = jnp.bfloat16
F32 = jnp.float32

_dot = functools.partial(jnp.dot, preferred_element_type=F32)


def _rmsnorm(x, g):
    return x * lax.rsqrt(jnp.mean(x * x, axis=-1, keepdims=True) + EPS) * g


def _sigmoid(x):
    return 1.0 / (1.0 + jnp.exp(-x))


def _gelu(z):
    return 0.5 * z * (1.0 + lax.erf(z * (1.0 / math.sqrt(2.0))))


def _interleave(n_sub, stages, skew=1):
    for wave in range(skew * (n_sub - 1) + len(stages)):
        for s in reversed(range(n_sub)):
            k = wave - skew * s
            if 0 <= k < len(stages):
                stages[k](s)


def _mixer_kernel(x_ref, g_ref, w_in_ref, bg_ref, lng_ref, lnb_ref, ws_ref, bsb_ref, conv_ref,
                  wg_ref, cs_ref, wa_ref, wb_ref, wc_ref, wo_ref, o_ref,
                  ybuf, cbuf, l1, l2, l3, *, tiles_per_batch):
    tm, d = x_ref.shape
    sub = MIXER_SUB
    aw = d // 2
    hd = aw // A_GROUPS
    nch = sub // CHUNK
    t_in_batch = pl.program_id(0) % tiles_per_batch

    @pl.when(t_in_batch == 0)
    def _():
        ybuf[0:CONV_HALO, :] = jnp.zeros((CONV_HALO, aw), F32)
        cbuf[0:POOL_HALO, :] = jnp.zeros((POOL_HALO, aw), F32)

    row = lax.broadcasted_iota(jnp.int32, (CHUNK, CHUNK), 0)
    col = lax.broadcasted_iota(jnp.int32, (CHUNK, CHUNK), 1)
    ws = [jnp.where(row >= col, ws_ref[g], 0.0).astype(BF16) for g in range(A_GROUPS)]
    zero = jnp.zeros((hd, hd), BF16)
    wbd = [jnp.concatenate([jnp.concatenate([wg_ref[gi], zero], axis=1),
                            jnp.concatenate([zero, wg_ref[gi + 1]], axis=1)], axis=0)
           for gi in range(0, len(POOL_WINDOWS), 2)]
    st = [dict() for _ in range(tm // sub)]

    def gate(h, k, lo, hi):
        zg = _dot(h, w_in_ref[:, 6 * aw + k * d + lo:6 * aw + k * d + hi])
        return _sigmoid(zg + bg_ref[k:k + 1, lo:hi])

    def in_proj(s):
        t = st[s]
        t["h"] = h = _rmsnorm(x_ref[s * sub:(s + 1) * sub, :], g_ref[...]).astype(BF16)
        t["zv"] = _dot(h, w_in_ref[:, aw:2 * aw])
        t["zu"] = _dot(h, w_in_ref[:, 0:aw])
        t["zb"] = _dot(h, w_in_ref[:, 2 * aw:5 * aw])

    def mix(s):
        t = st[s]
        h = t["h"]
        v = _gelu(t.pop("zv"))
        mu = jnp.mean(v, axis=-1, keepdims=True)
        vc = v - mu
        var = jnp.mean(vc * vc, axis=-1, keepdims=True)
        vb = (vc * lax.rsqrt(var + EPS) * lng_ref[...] + lnb_ref[...]).astype(BF16)
        mixed_g = []
        for g in range(A_GROUPS):
            vcat = jnp.concatenate(
                [vb[c * CHUNK:(c + 1) * CHUNK, g * hd:(g + 1) * hd] for c in range(nch)], axis=1)
            mixed_g.append(_dot(ws[g], vcat))
        zc = _dot(h, w_in_ref[:, 5 * aw:6 * aw])
        t["g0"] = gate(h, 0, 0, d)
        bias = bsb_ref[...]
        mixed = jnp.concatenate(
            [jnp.concatenate([mixed_g[g][:, c * hd:(c + 1) * hd] for g in range(A_GROUPS)], axis=1)
             + bias for c in range(nch)], axis=0)
        t["ua"] = (_gelu(t.pop("zu")) * mixed).astype(BF16)

        zb = t.pop("zb")
        y = zb[:, aw:2 * aw] * zb[:, 2 * aw:3 * aw]
        r0 = CONV_HALO + s * sub
        ybuf[r0:r0 + sub, :] = y
        conv = (conv_ref[0:1, :] * ybuf[r0 - 2:r0 - 2 + sub, :]
                + conv_ref[1:2, :] * ybuf[r0 - 1:r0 - 1 + sub, :]
                + conv_ref[2:3, :] * y)
        t["ybp"] = (zb[:, 0:aw] * conv).astype(BF16)

        r0 = POOL_HALO + s * sub
        r1 = r0 + sub
        cbuf[r0:r1, :] = zc
        lo = [8, 16, 24, 32] if s == 0 else [r0] * 4
        s2 = cbuf[lo[0]:r1, :] + cbuf[lo[0] - 1:r1 - 1, :]
        l1[lo[0]:r1, :] = s2[:, hd:]
        s4 = l1[lo[1]:r1, :] + l1[lo[1] - 2:r1 - 2, :]
        l2[lo[1]:r1, :] = s4[:, hd:]
        s8 = l2[lo[2]:r1, :] + l2[lo[2] - 4:r1 - 4, :]
        l3[lo[2]:r1, :] = s8[:, hd:]
        s16 = l3[lo[3]:r1, :] + l3[lo[3] - 8:r1 - 8, :]
        sums = (s2[r0 - lo[0]:, 0:hd], s4[r0 - lo[1]:, 0:hd], s8[r0 - lo[2]:, 0:hd], s16[r0 - lo[3]:, :])
        pos = t_in_batch * tm + s * sub + lax.broadcasted_iota(jnp.int32, (sub, hd), 0)
        pool = []
        for gi, win in enumerate(POOL_WINDOWS):
            inv_count = 1.0 / jnp.minimum(pos + 1, win).astype(F32)
            pool.append((sums[gi] * inv_count - zc[:, gi * hd:(gi + 1) * hd]).astype(BF16))
        t["pool"] = pool

    def branch_proj(s):
        t = st[s]
        h = t["h"]
        merged = t.pop("g0") * _dot(t.pop("ua"), wa_ref[...])
        yb = _dot(t.pop("ybp"), wb_ref[...])
        g1 = gate(h, 1, 0, d)
        pool = t.pop("pool")
        pooled = [_dot(jnp.concatenate(pool[2 * j:2 * j + 2], axis=1), wbd[j]) for j in range(len(wbd))]
        t["merged"] = merged + g1 * yb
        t["pc"] = (jnp.concatenate(pooled, axis=1) * cs_ref[...]).astype(BF16)

    def out_proj(s):
        t = st[s]
        h = t.pop("h")
        g2 = gate(h, 2, 0, d)
        yc = _dot(t.pop("pc"), wc_ref[...])
        merged = t.pop("merged") + g2 * yc
        rows = slice(s * sub, (s + 1) * sub)
        o_ref[rows, :] = x_ref[rows, :] + _dot(merged.astype(BF16), wo_ref[...])

    _interleave(len(st), (in_proj, mix, branch_proj, out_proj))

    ybuf[0:CONV_HALO, :] = ybuf[tm:tm + CONV_HALO, :]
    cbuf[0:POOL_HALO, :] = cbuf[tm:tm + POOL_HALO, :]


def _kv_kernel(mem_ref, g_ref, wkv_ref, k_ref, v_ref):
    d = mem_ref.shape[-1]
    m = _rmsnorm(mem_ref[...], g_ref[...]).astype(BF16)
    kv = _dot(m, wkv_ref[...])
    k_ref[...] = kv[:, :d].astype(BF16)
    v_ref[...] = kv[:, d:].astype(BF16)


def _xattn_kernel(x_ref, g_ref, wq_ref, k_ref, v_ref, wo_ref, o_ref):
    tm, d = x_ref.shape
    sub = XATTN_SUB
    dh = d // XATTN_HEADS
    head_cols = [slice(hh * dh, (hh + 1) * dh) for hh in range(XATTN_HEADS)]
    vals = [None] * (tm // sub)

    def q_proj(s):
        hq = _rmsnorm(x_ref[s * sub:(s + 1) * sub, :], g_ref[...]).astype(BF16)
        vals[s] = (_dot(hq, wq_ref[...]) * (dh ** -0.5)).astype(BF16)

    def probs(s):
        q = vals[s]
        out = []
        for sl in head_cols:
            sc = lax.dot_general(q[:, sl], k_ref[:, sl], (((1,), (1,)), ((), ())),
                                 preferred_element_type=F32)
            p = jnp.exp(sc - jnp.max(sc, axis=-1, keepdims=True))
            out.append((p.astype(BF16), 1.0 / jnp.sum(p, axis=-1, keepdims=True)))
        vals[s] = out

    def attend(s):
        heads = [_dot(p, v_ref[:, sl]) * inv_l for (p, inv_l), sl in zip(vals[s], head_cols)]
        vals[s] = jnp.concatenate(heads, axis=1).astype(BF16)

    def out_proj(s):
        rows = slice(s * sub, (s + 1) * sub)
        o_ref[rows, :] = x_ref[rows, :] + _dot(vals[s], wo_ref[...])
        vals[s] = None

    _interleave(len(vals), (q_proj, probs, attend, out_proj))


def _ffn_kernel(x_ref, g_ref, w_in_ref, w_out_ref, gf_ref, o_ref, *, final_norm):
    tm, _ = x_ref.shape
    sub = FFN_SUB
    hidden = w_out_ref.shape[0]
    acts = [None] * (tm // sub)

    def act(s):
        h = _rmsnorm(x_ref[s * sub:(s + 1) * sub, :], g_ref[...]).astype(BF16)
        gate = _dot(h, w_in_ref[:, 0:hidden])
        up = _dot(h, w_in_ref[:, hidden:2 * hidden])
        acts[s] = (gate * _sigmoid(gate) * up).astype(BF16)

    def out_proj(s):
        rows = slice(s * sub, (s + 1) * sub)
        out = x_ref[rows, :] + _dot(acts[s], w_out_ref[...])
        if final_norm:
            out = _rmsnorm(out, gf_ref[...])
        o_ref[rows, :] = out
        acts[s] = None

    _interleave(len(acts), (act, out_proj))


def _layer_spec(arr, l):
    nd = arr.ndim - 1
    return pl.BlockSpec((None,) + arr.shape[1:], lambda i: (l,) + (0,) * nd,
                        pipeline_mode=pl.Buffered(1))


def _compiler_params():
    return pltpu.CompilerParams(dimension_semantics=("arbitrary",),
                                vmem_limit_bytes=VMEM_LIMIT_BYTES)


def _mixer_call(x2, l, seq, g_mix, w_in, b_gate, ln_g, ln_b, ws, bsb, conv, wg, cs, wa, wb, wc, wo):
    n, d = x2.shape
    tm = MIXER_TILE
    aw = d // 2
    hd = aw // A_GROUPS
    tok = pl.BlockSpec((tm, d), lambda i: (i, 0))
    params = (g_mix, w_in, b_gate, ln_g, ln_b, ws, bsb, conv, wg, cs, wa, wb, wc, wo)
    pool_rows = tm + POOL_HALO
    return pl.pallas_call(
        functools.partial(_mixer_kernel, tiles_per_batch=seq // tm),
        grid=(n // tm,),
        in_specs=[tok] + [_layer_spec(p, l) for p in params],
        out_specs=tok,
        out_shape=jax.ShapeDtypeStruct((n, d), F32),
        scratch_shapes=[pltpu.VMEM((tm + CONV_HALO, aw), F32),
                        pltpu.VMEM((pool_rows, aw), F32),
                        pltpu.VMEM((pool_rows, aw - hd), F32),
                        pltpu.VMEM((pool_rows, aw - 2 * hd), F32),
                        pltpu.VMEM((pool_rows, aw - 3 * hd), F32)],
        compiler_params=_compiler_params(),
        name=f"mixer_l{l}",
    )(x2, *params)


def _kv_call(mem, l, g_mem, w_kv):
    b, m, d = mem.shape
    blk = pl.BlockSpec((None, m, d), lambda i: (i, 0, 0))
    return pl.pallas_call(
        _kv_kernel,
        grid=(b,),
        in_specs=[blk, _layer_spec(g_mem, l), _layer_spec(w_kv, l)],
        out_specs=[blk, blk],
        out_shape=[jax.ShapeDtypeStruct((b, m, d), BF16)] * 2,
        compiler_params=_compiler_params(),
        name=f"kv_l{l}",
    )(mem, g_mem, w_kv)


def _xattn_call(x2, l, seq, k, v, g_x, wq, wo):
    n, d = x2.shape
    tm = XATTN_TILE
    tiles_per_batch = seq // tm
    tok = pl.BlockSpec((tm, d), lambda i: (i, 0))
    kvb = pl.BlockSpec((None,) + k.shape[1:], lambda i: (i // tiles_per_batch, 0, 0))
    return pl.pallas_call(
        _xattn_kernel,
        grid=(n // tm,),
        in_specs=[tok, _layer_spec(g_x, l), _layer_spec(wq, l), kvb, kvb, _layer_spec(wo, l)],
        out_specs=tok,
        out_shape=jax.ShapeDtypeStruct((n, d), F32),
        compiler_params=_compiler_params(),
        name=f"xattn_l{l}",
    )(x2, g_x, wq, k, v, wo)


def _ffn_call(x2, l, g_ffn, w_in, w_out, g_final, final_norm):
    n, d = x2.shape
    tm = FFN_TILE
    tok = pl.BlockSpec((tm, d), lambda i: (i, 0))
    gf =pl.BlockSpec(g_final.shape, lambda i: (0, 0), pipeline_mode=pl.Buffered(1))
    return pl.pallas_call(
        functools.partial(_ffn_kernel, final_norm=final_norm),
        grid=(n // tm,),
        in_specs=[tok, _layer_spec(g_ffn, l), _layer_spec(w_in, l), _layer_spec(w_out, l), gf],
        out_specs=tok,
        out_shape=jax.ShapeDtypeStruct((n, d), F32),
        compiler_params=_compiler_params(),
        name=f"ffn_l{l}",
    )(x2, g_ffn, w_in, w_out, g_final)


def kernel(x, mem, g_mix, w_in, b_gate, a_ln_g, a_ln_b, a_ws, a_bs, b_conv, c_wg, c_scale,
           w_branch_a, w_branch_b, w_branch_c, w_o, g_xattn, g_mem, w_xq, w_xkv, w_xo,
           g_ffn, w_ffn_in, w_ffn_out, g_final):
    b, s, d = x.shape
    depth = w_in.shape[0]
    assert s % XATTN_TILE == 0 and XATTN_TILE % XATTN_SUB == 0
    assert s % FFN_TILE == 0 and FFN_TILE % FFN_SUB == 0
    assert s % MIXER_TILE == 0 and MIXER_TILE % MIXER_SUB == 0 and MIXER_SUB % CHUNK == 0
    assert MIXER_SUB >= POOL_HALO
    assert d // 2 == A_GROUPS * LANES == len(POOL_WINDOWS) * LANES and a_ws.shape[-1] == CHUNK

    row = lambda p: p[:, None, :]
    bsb = jnp.repeat(jnp.swapaxes(a_bs, 1, 2), LANES, axis=2)
    w_in_b, wa_b, wb_b, wc_b, wo_b = (w.astype(BF16) for w in (w_in, w_branch_a, w_branch_b, w_branch_c, w_o))
    wg_b, wq_b, wkv_b, wxo_b = (w.astype(BF16) for w in (c_wg, w_xq, w_xkv, w_xo))
    wfi_b, wfo_b = w_ffn_in.astype(BF16), w_ffn_out.astype(BF16)

    x2 = x.reshape(b * s, d)
    for l in range(depth):
        x2 = _mixer_call(x2, l, s, row(g_mix), w_in_b, b_gate, row(a_ln_g), row(a_ln_b), a_ws, bsb,
                         b_conv, wg_b, row(c_scale), wa_b, wb_b, wc_b, wo_b)
        k, v = _kv_call(mem, l, row(g_mem), wkv_b)
        x2 = _xattn_call(x2, l, s, k, v, row(g_xattn), wq_b, wxo_b)
        x2 = _ffn_call(x2, l, row(g_ffn), wfi_b, wfo_b, g_final[None, :], l == depth - 1)
    return x2.reshape(b, s, d)
```

```python
import functools
import math

import jax
import jax.numpy as jnp
from jax import lax
from jax.experimental import pallas as pl
from jax.experimental.pallas import tpu as pltpu

EPS = 1e-6
CHUNK = 128
A_GROUPS = 4
POOL_WINDOWS = (2, 4, 8, 16)
CONV_WIDTH = 3
XATTN_HEADS = 4

LANES = 128
SUBLANES = 8
V7X_VMEM_BYTES = 64 * 1024 * 1024
VMEM_LIMIT_BYTES = V7X_VMEM_BYTES - 4 * 1024 * 1024

MIXER_SUB = 512
MIXER_TILE = 1024
XATTN_SUB = 512
XATTN_TILE = 2048
FFN_SUB = 512
FFN_TILE = 1024
POOL_HALO = 32
CONV_HALO = SUBLANES

BF16 = jnp.bfloat16
F32 = jnp.float32

_dot = functools.partial(jnp.dot, preferred_element_type=F32)


def _rmsnorm(x, g):
    return x * lax.rsqrt(jnp.mean(x * x, axis=-1, keepdims=True) + EPS) * g


def _sigmoid(x):
    return 1.0 / (1.0 + jnp.exp(-x))


def _gelu(z):
    return 0.5 * z * (1.0 + lax.erf(z * (1.0 / math.sqrt(2.0))))


def _interleave(n_sub, stages, skew=1):
    for wave in range(skew * (n_sub - 1) + len(stages)):
        for s in reversed(range(n_sub)):
            k = wave - skew * s
            if 0 <= k < len(stages):
                stages[k](s)


def _mixer_kernel(x_ref, g_ref, w_in_ref, bg_ref, lng_ref, lnb_ref, ws_ref, bsb_ref, conv_ref,
                  wg_ref, cs_ref, wa_ref, wb_ref, wc_ref, wo_ref, o_ref,
                  ybuf, cbuf, l1, l2, l3, *, tiles_per_batch):
    tm, d = x_ref.shape
    sub = MIXER_SUB
    aw = d // 2
    hd = aw // A_GROUPS
    nch = sub // CHUNK
    t_in_batch = pl.program_id(0) % tiles_per_batch

    @pl.when(t_in_batch == 0)
    def _():
        ybuf[0:CONV_HALO, :] = jnp.zeros((CONV_HALO, aw), F32)
        cbuf[0:POOL_HALO, :] = jnp.zeros((POOL_HALO, aw), F32)

    row = lax.broadcasted_iota(jnp.int32, (CHUNK, CHUNK), 0)
    col = lax.broadcasted_iota(jnp.int32, (CHUNK, CHUNK), 1)
    ws = [jnp.where(row >= col, ws_ref[g], 0.0).astype(BF16) for g in range(A_GROUPS)]
    zero = jnp.zeros((hd, hd), BF16)
    wbd = [jnp.concatenate([jnp.concatenate([wg_ref[gi], zero], axis=1),
                            jnp.concatenate([zero, wg_ref[gi + 1]], axis=1)], axis=0)
           for gi in range(0, len(POOL_WINDOWS), 2)]
    st = [dict() for _ in range(tm // sub)]

    def gate(h, k, lo, hi):
        zg = _dot(h, w_in_ref[:, 6 * aw + k * d + lo:6 * aw + k * d + hi])
        return _sigmoid(zg + bg_ref[k:k + 1, lo:hi])

    def in_proj(s):
        t = st[s]
        t["h"] = h = _rmsnorm(x_ref[s * sub:(s + 1) * sub, :], g_ref[...]).astype(BF16)
        t["zv"] = _dot(h, w_in_ref[:, aw:2 * aw])
        t["zu"] = _dot(h, w_in_ref[:, 0:aw])
        t["zb"] = _dot(h, w_in_ref[:, 2 * aw:5 * aw])

    def mix(s):
        t = st[s]
        h = t["h"]
        v = _gelu(t.pop("zv"))
        mu = jnp.mean(v, axis=-1, keepdims=True)
        vc = v - mu
        var = jnp.mean(vc * vc, axis=-1, keepdims=True)
        vb = (vc * lax.rsqrt(var + EPS) * lng_ref[...] + lnb_ref[...]).astype(BF16)
        mixed_g = []
        for g in range(A_GROUPS):
            vcat = jnp.concatenate(
                [vb[c * CHUNK:(c + 1) * CHUNK, g * hd:(g + 1) * hd] for c in range(nch)], axis=1)
            mixed_g.append(_dot(ws[g], vcat))
        zc = _dot(h, w_in_ref[:, 5 * aw:6 * aw])
        t["g0"] = gate(h, 0, 0, d)
        bias = bsb_ref[...]
        mixed = jnp.concatenate(
            [jnp.concatenate([mixed_g[g][:, c * hd:(c + 1) * hd] for g in range(A_GROUPS)], axis=1)
             + bias for c in range(nch)], axis=0)
        t["ua"] = (_gelu(t.pop("zu")) * mixed).astype(BF16)

        zb = t.pop("zb")
        y = zb[:, aw:2 * aw] * zb[:, 2 * aw:3 * aw]
        r0 = CONV_HALO + s * sub
        ybuf[r0:r0 + sub, :] = y
        conv = (conv_ref[0:1, :] * ybuf[r0 - 2:r0 - 2 + sub, :]
                + conv_ref[1:2, :] * ybuf[r0 - 1:r0 - 1 + sub, :]
                + conv_ref[2:3, :] * y)
        t["ybp"] = (zb[:, 0:aw] * conv).astype(BF16)

        r0 = POOL_HALO + s * sub
        r1 = r0 + sub
        cbuf[r0:r1, :] = zc
        lo = [8, 16, 24, 32] if s == 0 else [r0] * 4
        s2 = cbuf[lo[0]:r1, :] + cbuf[lo[0] - 1:r1 - 1, :]
        l1[lo[0]:r1, :] = s2[:, hd:]
        s4 = l1[lo[1]:r1, :] + l1[lo[1] - 2:r1 - 2, :]
        l2[lo[1]:r1, :] = s4[:, hd:]
        s8 = l2[lo[2]:r1, :] + l2[lo[2] - 4:r1 - 4, :]
        l3[lo[2]:r1, :] = s8[:, hd:]
        s16 = l3[lo[3]:r1, :] + l3[lo[3] - 8:r1 - 8, :]
        sums = (s2[r0 - lo[0]:, 0:hd], s4[r0 - lo[1]:, 0:hd], s8[r0 - lo[2]:, 0:hd], s16[r0 - lo[3]:, :])
        pos = t_in_batch * tm + s * sub + lax.broadcasted_iota(jnp.int32, (sub, hd), 0)
        pool = []
        for gi, win in enumerate(POOL_WINDOWS):
            inv_count = 1.0 / jnp.minimum(pos + 1, win).astype(F32)
            pool.append((sums[gi] * inv_count - zc[:, gi * hd:(gi + 1) * hd]).astype(BF16))
        t["pool"] = pool

    def branch_proj(s):
        t = st[s]
        h = t["h"]
        merged = t.pop("g0") * _dot(t.pop("ua"), wa_ref[...])
        yb = _dot(t.pop("ybp"), wb_ref[...])
        g1 = gate(h, 1, 0, d)
        pool = t.pop("pool")
        pooled = [_dot(jnp.concatenate(pool[2 * j:2 * j + 2], axis=1), wbd[j]) for j in range(len(wbd))]
        t["merged"] = merged + g1 * yb
        t["pc"] = (jnp.concatenate(pooled, axis=1) * cs_ref[...]).astype(BF16)

    def out_proj(s):
        t = st[s]
        h = t.pop("h")
        g2 = gate(h, 2, 0, d)
        yc = _dot(t.pop("pc"), wc_ref[...])
        merged = t.pop("merged") + g2 * yc
        rows = slice(s * sub, (s + 1) * sub)
        o_ref[rows, :] = x_ref[rows, :] + _dot(merged.astype(BF16), wo_ref[...])

    _interleave(len(st), (in_proj, mix, branch_proj, out_proj))

    ybuf[0:CONV_HALO, :] = ybuf[tm:tm + CONV_HALO, :]
    cbuf[0:POOL_HALO, :] = cbuf[tm:tm + POOL_HALO, :]


def _kv_kernel(mem_ref, g_ref, wkv_ref, k_ref, v_ref):
    d = mem_ref.shape[-1]
    m = _rmsnorm(mem_ref[...], g_ref[...]).astype(BF16)
    kv = _dot(m, wkv_ref[...])
    k_ref[...] = kv[:, :d].astype(BF16)
    v_ref[...] = kv[:, d:].astype(BF16)


def _xattn_kernel(x_ref, g_ref, wq_ref, k_ref, v_ref, wo_ref, o_ref):
    tm, d = x_ref.shape
    sub = XATTN_SUB
    dh = d // XATTN_HEADS
    head_cols = [slice(hh * dh, (hh + 1) * dh) for hh in range(XATTN_HEADS)]
    vals = [None] * (tm // sub)

    def q_proj(s):
        hq = _rmsnorm(x_ref[s * sub:(s + 1) * sub, :], g_ref[...]).astype(BF16)
        vals[s] = (_dot(hq, wq_ref[...]) * (dh ** -0.5)).astype(BF16)

    def probs(s):
        q = vals[s]
        out = []
        for sl in head_cols:
            sc = lax.dot_general(q[:, sl], k_ref[:, sl], (((1,), (1,)), ((), ())),
                                 preferred_element_type=F32)
            p = jnp.exp(sc - jnp.max(sc, axis=-1, keepdims=True))
            out.append((p.astype(BF16), 1.0 / jnp.sum(p, axis=-1, keepdims=True)))
        vals[s] = out

    def attend(s):
        heads = [_dot(p, v_ref[:, sl]) * inv_l for (p, inv_l), sl in zip(vals[s], head_cols)]
        vals[s] = jnp.concatenate(heads, axis=1).astype(BF16)

    def out_proj(s):
        rows = slice(s * sub, (s + 1) * sub)
        o_ref[rows, :] = x_ref[rows, :] + _dot(vals[s], wo_ref[...])
        vals[s] = None

    _interleave(len(vals), (q_proj, probs, attend, out_proj))


def _ffn_kernel(x_ref, g_ref, w_in_ref, w_out_ref, gf_ref, o_ref, *, final_norm):
    tm, _ = x_ref.shape
    sub = FFN_SUB
    hidden = w_out_ref.shape[0]
    acts = [None] * (tm // sub)

    def act(s):
        h = _rmsnorm(x_ref[s * sub:(s + 1) * sub, :], g_ref[...]).astype(BF16)
        gate = _dot(h, w_in_ref[:, 0:hidden])
        up = _dot(h, w_in_ref[:, hidden:2 * hidden])
        acts[s] = (gate * _sigmoid(gate) * up).astype(BF16)

    def out_proj(s):
        rows = slice(s * sub, (s + 1) * sub)
        out = x_ref[rows, :] + _dot(acts[s], w_out_ref[...])
        if final_norm:
            out = _rmsnorm(out, gf_ref[...])
        o_ref[rows, :] = out
        acts[s] = None

    _interleave(len(acts), (act, out_proj))


def _layer_spec(arr, l):
    nd = arr.ndim - 1
    return pl.BlockSpec((None,) + arr.shape[1:], lambda i: (l,) + (0,) * nd,
                        pipeline_mode=pl.Buffered(1))


def _compiler_params():
    return pltpu.CompilerParams(dimension_semantics=("arbitrary",),
                                vmem_limit_bytes=VMEM_LIMIT_BYTES)


def _mixer_call(x2, l, seq, g_mix, w_in, b_gate, ln_g, ln_b, ws, bsb, conv, wg, cs, wa, wb, wc, wo):
    n, d = x2.shape
    tm = MIXER_TILE
    aw = d // 2
    hd = aw // A_GROUPS
    tok = pl.BlockSpec((tm, d), lambda i: (i, 0))
    params = (g_mix, w_in, b_gate, ln_g, ln_b, ws, bsb, conv, wg, cs, wa, wb, wc, wo)
    pool_rows = tm + POOL_HALO
    return pl.pallas_call(
        functools.partial(_mixer_kernel, tiles_per_batch=seq // tm),
        grid=(n // tm,),
        in_specs=[tok] + [_layer_spec(p, l) for p in params],
        out_specs=tok,
        out_shape=jax.ShapeDtypeStruct((n, d), F32),
        scratch_shapes=[pltpu.VMEM((tm + CONV_HALO, aw), F32),
                        pltpu.VMEM((pool_rows, aw), F32),
                        pltpu.VMEM((pool_rows, aw - hd), F32),
                        pltpu.VMEM((pool_rows, aw - 2 * hd), F32),
                        pltpu.VMEM((pool_rows, aw - 3 * hd), F32)],
        compiler_params=_compiler_params(),
        name=f"mixer_l{l}",
    )(x2, *params)


def _kv_call(mem, g_mem, w_kv):
    b, m, d = mem.shape
    depth = w_kv.shape[0]
    per_layer = lambda arr: pl.BlockSpec((None,) + arr.shape[1:], lambda i: (i // b, 0, 0))
    out = pl.BlockSpec((None, None, m, d), lambda i: (i // b, i % b, 0, 0))
    return pl.pallas_call(
        _kv_kernel,
        grid=(depth * b,),
        in_specs=[pl.BlockSpec((None, m, d), lambda i: (i % b, 0, 0)), per_layer(g_mem), per_layer(w_kv)],
        out_specs=[out, out],
        out_shape=[jax.ShapeDtypeStruct((depth, b, m, d), BF16)] * 2,
        compiler_params=_compiler_params(),
        name="kv",
    )(mem, g_mem, w_kv)


def _xattn_call(x2, l, seq, k, v, g_x, wq, wo):
    n, d = x2.shape
    tm = XATTN_TILE
    tiles_per_batch = seq // tm
    tok = pl.BlockSpec((tm, d), lambda i: (i, 0))
    kvb = pl.BlockSpec((None, None) + k.shape[2:], lambda i: (l, i // tiles_per_batch, 0, 0))
    return pl.pallas_call(
        _xattn_kernel,
        grid=(n // tm,),
        in_specs=[tok, _layer_spec(g_x, l), _layer_spec(wq, l), kvb, kvb, _layer_spec(wo, l)],
        out_specs=tok,
        out_shape=jax.ShapeDtypeStruct((n, d), F32),
        compiler_params=_compiler_params(),
        name=f"xattn_l{l}",
    )(x2, g_x, wq, k, v, wo)


def _ffn_call(x2, l, g_ffn, w_in, w_out, g_final, final_norm):
    n, d = x2.shape
    tm = FFN_TILE
    tok = pl.BlockSpec((tm, d), lambda i: (i, 0))
    gf =pl.BlockSpec(g_final.shape, lambda i: (0, 0), pipeline_mode=pl.Buffered(1))
    return pl.pallas_call(
        functools.partial(_ffn_kernel, final_norm=final_norm),
        grid=(n // tm,),
        in_specs=[tok, _layer_spec(g_ffn, l), _layer_spec(w_in, l), _layer_spec(w_out, l), gf],
        out_specs=tok,
        out_shape=jax.ShapeDtypeStruct((n, d), F32),
        compiler_params=_compiler_params(),
        name=f"ffn_l{l}",
    )(x2, g_ffn, w_in, w_out, g_final)


def kernel(x, mem, g_mix, w_in, b_gate, a_ln_g, a_ln_b, a_ws, a_bs, b_conv, c_wg, c_scale,
           w_branch_a, w_branch_b, w_branch_c, w_o, g_xattn, g_mem, w_xq, w_xkv, w_xo,
           g_ffn, w_ffn_in, w_ffn_out, g_final):
    b, s, d = x.shape
    depth = w_in.shape[0]
    assert s % XATTN_TILE == 0 and XATTN_TILE % XATTN_SUB == 0
    assert s % FFN_TILE == 0 and FFN_TILE % FFN_SUB == 0
    assert s % MIXER_TILE == 0 and MIXER_TILE % MIXER_SUB == 0 and MIXER_SUB % CHUNK == 0
    assert MIXER_SUB >= POOL_HALO
    assert d // 2 == A_GROUPS * LANES == len(POOL_WINDOWS) * LANES and a_ws.shape[-1] == CHUNK

    row = lambda p: p[:, None, :]
    bsb = jnp.repeat(jnp.swapaxes(a_bs, 1, 2), LANES, axis=2)
    w_in_b, wa_b, wb_b, wc_b, wo_b = (w.astype(BF16) for w in (w_in, w_branch_a, w_branch_b, w_branch_c, w_o))
    wg_b, wq_b, wkv_b, wxo_b = (w.astype(BF16) for w in (c_wg, w_xq, w_xkv, w_xo))
    wfi_b, wfo_b = w_ffn_in.astype(BF16), w_ffn_out.astype(BF16)

    x2 = x.reshape(b * s, d)
    k, v = _kv_call(mem, row(g_mem), wkv_b)
    for l in range(depth):
        x2 = _mixer_call(x2, l, s, row(g_mix), w_in_b, b_gate, row(a_ln_g), row(a_ln_b), a_ws, bsb,
                         b_conv, wg_b, row(c_scale), wa_b, wb_b, wc_b, wo_b)
        x2 = _xattn_call(x2, l, s, k, v, row(g_xattn), wq_b, wxo_b)
        x2 = _ffn_call(x2, l, row(g_ffn), wfi_b, wfo_b, g_final[None, :], l == depth - 1)
    return x2.reshape(b, s, d)
```
